```python
import math
import jax
import jax.numpy as jnp
from jax import lax
import numpy as np

D_MODEL = 1024
BATCH = 32
SEQ = 2048
DEPTH = 2
DEC_BATCH = 128
DEC_SEQ = 1
PAST_LEN = 16384
PAGE_SIZE = 128

N_MIXERS = 4
GROUP_W = D_MODEL // N_MIXERS
SB_HEADS = 4
SB_DIM = GROUP_W // SB_HEADS
MLA_HEADS = 4
MLA_V = GROUP_W // MLA_HEADS
MLA_NOPE = 64
MLA_ROPE = 32
MLA_Q_RANK = 3 * D_MODEL // 16
MLA_KV_RANK = D_MODEL // 8
DIFF_HEADS = 4
DIFF_DIM = GROUP_W // DIFF_HEADS // 2
FOX_HEADS = 4
FOX_DIM = GROUP_W // FOX_HEADS
FOX_BIAS_INIT = 3.0
Q_BLOCK = 128
N_BUCKETS = 32
REL_MAX_DIST = 128
ROPE_THETA = 10000.0
D_FF = 256 * ((8 * D_MODEL // 3 + 255) // 256)
N_EXPERTS = 8
TOP_K = 2
D_FF_EXPERT = 7 * D_MODEL // 2
MOE_BLOCK = 128
EPS = 1e-6
NEG_INF = -1e30
F32 = jnp.float32
IN_WIDTHS = (SB_HEADS * SB_DIM, SB_DIM, SB_DIM,
             MLA_Q_RANK, MLA_KV_RANK, MLA_ROPE,
             DIFF_HEADS * 2 * DIFF_DIM, 2 * DIFF_DIM, 2 * DIFF_DIM,
             FOX_HEADS * FOX_DIM, FOX_DIM, FOX_DIM, FOX_HEADS)

kernel_name = 'hybrid_parallel_heads_decode_step'


def rmsnorm(x, g):
    xf = x.astype(F32)
    y = xf * lax.rsqrt(jnp.mean(xf * xf, axis=-1, keepdims=True) + EPS)
    return (y * g.astype(F32)).astype(x.dtype)


def rope_angles(pos):
    half = MLA_ROPE // 2
    inv = ROPE_THETA ** (-jnp.arange(half, dtype=F32) / half)
    ang = pos.astype(F32)[:, None] * inv[None, :]
    return jnp.cos(ang), jnp.sin(ang)


def apply_rope(x, cos, sin):
    half = MLA_ROPE // 2
    x1 = x[..., :half].astype(F32)
    x2 = x[..., half:].astype(F32)
    return jnp.concatenate([x1 * cos - x2 * sin, x2 * cos + x1 * sin], axis=-1).astype(x.dtype)


def t5_bucket(rel):
    n = jnp.maximum(rel, 0)
    exact = N_BUCKETS // 2
    log_ratio = jnp.log(jnp.maximum(n, 1).astype(F32) / exact) / math.log(REL_MAX_DIST / exact)
    far = exact + (log_ratio * (N_BUCKETS - exact)).astype(jnp.int32)
    return jnp.where(n < exact, n, jnp.minimum(far, N_BUCKETS - 1))


def stick_breaking_core(q, q_pos, k, v, k_pos):
    z = jnp.einsum('bthd,bsd->bhts', q, k).astype(F32) * (SB_DIM ** -0.5)
    mask = k_pos[None, :] < q_pos[:, None]
    log_beta = jax.nn.log_sigmoid(z)
    log_1m = jnp.where(mask, jax.nn.log_sigmoid(-z), 0.0)
    after = lax.cumsum(log_1m, axis=3, reverse=True) - log_1m
    w = jnp.where(mask, jnp.exp(log_beta + after), 0.0)
    return jnp.einsum('bhts,bsd->bthd', w.astype(v.dtype), v)


def mla_core(q_lat, q_rope, q_pos, ckv, kr, k_pos):
    s = (jnp.einsum('bthr,bsr->bhts', q_lat, ckv) + jnp.einsum('bthe,bse->bhts', q_rope, kr)).astype(F32)
    s = s * ((MLA_NOPE + MLA_ROPE) ** -0.5)
    mask = k_pos[None, :] <= q_pos[:, None]
    p = jax.nn.softmax(jnp.where(mask, s, NEG_INF), axis=-1)
    return jnp.einsum('bhts,bsr->bthr', p.astype(ckv.dtype), ckv)


def diff_core(q, q_pos, k, v, k_pos, rel_table, lam):
    s = jnp.einsum('bthid,bsid->bhits', q, k).astype(F32) * (DIFF_DIM ** -0.5)
    bucket = t5_bucket(q_pos[:, None] - k_pos[None, :])
    bias = jnp.moveaxis(rel_table.astype(F32)[bucket], -1, 0)
    s = s + bias[None, :, None]
    mask = k_pos[None, :] <= q_pos[:, None]
    p = jax.nn.softmax(jnp.where(mask, s, NEG_INF), axis=-1)
    a = p[:, :, 0] - lam * p[:, :, 1]
    return jnp.einsum('bhts,bse->bthe', a.astype(v.dtype), v)


def fox_core(q, cum_q, q_pos, k, v, cum_k, k_pos):
    s = jnp.einsum('bthd,bsd->bhts', q, k).astype(F32) * (FOX_DIM ** -0.5)
    s = s + jnp.swapaxes(cum_q, 1, 2)[..., :, None] - jnp.swapaxes(cum_k, 1, 2)[..., None, :]
    mask = k_pos[None, :] <= q_pos[:, None]
    p = jax.nn.softmax(jnp.where(mask, s, NEG_INF), axis=-1)
    return jnp.einsum('bhts,bsd->bthd', p.astype(v.dtype), v)


def map_query_blocks(core, q_args, q_pos):
    t_len = q_pos.shape[0]
    n_blk = t_len // Q_BLOCK
    def to_blocks(a):
        return jnp.moveaxis(a.reshape((a.shape[0], n_blk, Q_BLOCK) + a.shape[2:]), 1, 0)
    blocks = tuple(to_blocks(a) for a in q_args)
    out = lax.map(lambda bp: core(bp[0], bp[1]), (blocks, q_pos.reshape(n_blk, Q_BLOCK)))
    out = jnp.moveaxis(out, 0, 1)
    return out.reshape((out.shape[0], t_len) + out.shape[3:])


def gather_pages(cache, layer, page_table):
    pages = cache[layer, page_table]
    return pages.reshape((pages.shape[0], pages.shape[1] * pages.shape[2]) + pages.shape[3:])


def token_mixers(h, pos, past, layer, rel_table, w_in, g_cq, w_uq, g_ckv, w_uk, w_uv,
                 lam_qk, g_subln, b_f, g_grp, w_out):
    bsz, t_len, _ = h.shape
    cuts = np.cumsum(IN_WIDTHS)[:-1].tolist()
    (qa, ka, va, cq, ckv, kr, qc, kc, vc, qd, kd, vd, fd) = jnp.split(h @ w_in, cuts, axis=-1)
    qa = qa.reshape(bsz, t_len, SB_HEADS, SB_DIM)
    cos, sin = rope_angles(pos)
    cq = rmsnorm(cq, g_cq)
    qb = jnp.einsum('btr,rhe->bthe', cq, w_uq)
    qb_lat = jnp.einsum('bthe,rhe->bthr', qb[..., :MLA_NOPE], w_uk)
    qb_rope = apply_rope(qb[..., MLA_NOPE:], cos[:, None, :], sin[:, None, :])
    ckv = rmsnorm(ckv, g_ckv)
    kr = apply_rope(kr, cos, sin)
    qc = qc.reshape(bsz, t_len, DIFF_HEADS, 2, DIFF_DIM)
    kc = kc.reshape(bsz, t_len, 2, DIFF_DIM)
    lam_init = 0.8 - 0.6 * math.exp(-0.3 * layer)
    lq = lam_qk.astype(F32)
    lam = jnp.exp(jnp.sum(lq[0] * lq[1])) - jnp.exp(jnp.sum(lq[2] * lq[3])) + lam_init
    qd = qd.reshape(bsz, t_len, FOX_HEADS, FOX_DIM)
    logf = jax.nn.log_sigmoid(fd.astype(F32) + b_f.astype(F32))

    new_rows = (ka, va, ckv, kr, kc, vc, kd, vd, logf)
    if past is None:
        keys, k_pos = new_rows, pos
        sweep = lambda core, q_args: map_query_blocks(core, q_args, pos)
    else:
        keys = tuple(jnp.concatenate([p.astype(r.dtype), r], axis=1) for p, r in zip(past, new_rows))
        k_pos = jnp.arange(keys[0].shape[1], dtype=jnp.int32)
        sweep = lambda core, q_args: core(q_args, pos)
    ka_, va_, ckv_, kr_, kc_, vc_, kd_, vd_, logf_ = keys
    cum_k = jnp.cumsum(logf_, axis=1)
    cum_q = cum_k[:, -t_len:]

    oa = sweep(lambda qs, qp: stick_breaking_core(qs[0], qp, ka_, va_, k_pos), (qa,))
    o_lat = sweep(lambda qs, qp: mla_core(qs[0], qs[1], qp, ckv_, kr_, k_pos), (qb_lat, qb_rope))
    ob = jnp.einsum('bthr,rhe->bthe', o_lat, w_uv)
    oc = sweep(lambda qs, qp: diff_core(qs[0], qp, kc_, vc_, k_pos, rel_table, lam), (qc,))
    od = sweep(lambda qs, qp: fox_core(qs[0], qs[1], qp, kd_, vd_, cum_k, k_pos), (qd, cum_q))

    oa = rmsnorm(oa.reshape(bsz, t_len, GROUP_W), g_grp[0])
    ob = rmsnorm(ob.reshape(bsz, t_len, GROUP_W), g_grp[1])
    oc = (rmsnorm(oc, g_subln) * (1.0 - lam_init)).reshape(bsz, t_len, GROUP_W)
    od = rmsnorm(od.reshape(bsz, t_len, GROUP_W), g_grp[2])
    y = jnp.concatenate([oa, ob, oc, od], axis=-1) @ w_out
    return y, new_rows


def swiglu(x, w1, w3, w2):
    return (jax.nn.silu(x @ w1) * (x @ w3)) @ w2


def moe_swiglu(x, w_router, w1, w3, w2):
    n, d = x.shape
    logits = jnp.dot(x.astype(F32), w_router.astype(F32))
    top_logit, top_e = lax.top_k(logits, TOP_K)
    gate = jax.nn.softmax(top_logit, axis=-1).astype(x.dtype)
    n_assign = n * TOP_K
    flat_e = top_e.reshape(n_assign).astype(jnp.int32)
    flat_t = jnp.repeat(jnp.arange(n, dtype=jnp.int32), TOP_K)
    flat_g = gate.reshape(n_assign)
    se, st, sg = lax.sort((flat_e, flat_t, flat_g), num_keys=1, is_stable=True)
    counts = jnp.zeros((N_EXPERTS,), jnp.int32).at[flat_e].add(1)
    start = jnp.cumsum(counts) - counts
    padded = (counts + MOE_BLOCK - 1) // MOE_BLOCK * MOE_BLOCK
    pend = jnp.cumsum(padded)
    pstart = pend - padded
    dest = pstart[se] + jnp.arange(n_assign, dtype=jnp.int32) - start[se]
    n_blocks = -(-n_assign // MOE_BLOCK) + N_EXPERTS
    n_rows = n_blocks * MOE_BLOCK
    row_tok = jnp.zeros((n_rows,), jnp.int32).at[dest].set(st)
    row_gate = jnp.zeros((n_rows,), x.dtype).at[dest].set(sg)
    block_start = jnp.arange(n_blocks, dtype=jnp.int32) * MOE_BLOCK
    block_e = jnp.minimum(jnp.searchsorted(pend, block_start, side='right'), N_EXPERTS - 1)
    def expert_block(args):
        tok, g, e = args
        hb = x[tok]
        return ((jax.nn.silu(hb @ w1[e]) * (hb @ w3[e])) @ w2[e]) * g[:, None]
    out = lax.map(expert_block, (row_tok.reshape(n_blocks, MOE_BLOCK),
                                 row_gate.reshape(n_blocks, MOE_BLOCK), block_e))
    return jax.ops.segment_sum(out.reshape(n_rows, d), row_tok, num_segments=n)


def channel_mixer(h, layer, w1_dense, w3_dense, w2_dense, w_router, w1_exp, w3_exp, w2_exp):
    i = layer // 2
    if layer % 2 == 0:
        return swiglu(h, w1_dense[i], w3_dense[i], w2_dense[i])
    b, t, d = h.shape
    return moe_swiglu(h.reshape(b * t, d), w_router[i], w1_exp[i], w3_exp[i], w2_exp[i]).reshape(b, t, d)


def setup_inputs(seed: int = 0) -> dict:
    key = jax.random.key(seed)
    keys = iter(jax.random.split(key, 48))
    def nrm(shape, scale=1.0):
        return jax.random.normal(next(keys), shape, F32) * scale
    def gain(shape):
        return 1.0 + 0.05 * nrm(shape)
    n_pages = PAST_LEN // PAGE_SIZE
    n_used = DEC_BATCH * n_pages
    n_pool = n_used + n_used // 4
    def pool(*feat):
        return nrm((DEPTH, n_pool, PAGE_SIZE) + feat)
    n_in = sum(IN_WIDTHS)
    n_dense = (DEPTH + 1) // 2
    n_moe = DEPTH // 2
    return {
        'x_prompt': nrm((BATCH, SEQ, D_MODEL)),
        'x_sample': nrm((DEC_BATCH, DEC_SEQ, D_MODEL)),
        'cache_sb_k': pool(SB_DIM),
        'cache_sb_v': pool(SB_DIM),
        'cache_mla_ckv': pool(MLA_KV_RANK),
        'cache_mla_kr': pool(MLA_ROPE),
        'cache_diff_k': pool(2, DIFF_DIM),
        'cache_diff_v': pool(2 * DIFF_DIM),
        'cache_fox_k': pool(FOX_DIM),
        'cache_fox_v': pool(FOX_DIM),
        'cache_fox_logf': jax.nn.log_sigmoid(pool(FOX_HEADS) + FOX_BIAS_INIT),
        'page_table': jax.random.permutation(next(keys), n_pool)[:n_used].reshape(DEC_BATCH, n_pages).astype(jnp.int32),
        'g_attn': gain((DEPTH, D_MODEL)),
        'w_in': nrm((DEPTH, D_MODEL, n_in), D_MODEL ** -0.5),
        'g_cq': gain((DEPTH, MLA_Q_RANK)),
        'w_uq': nrm((DEPTH, MLA_Q_RANK, MLA_HEADS, MLA_NOPE + MLA_ROPE), MLA_Q_RANK ** -0.5),
        'g_ckv': gain((DEPTH, MLA_KV_RANK)),
        'w_uk': nrm((DEPTH, MLA_KV_RANK, MLA_HEADS, MLA_NOPE), MLA_KV_RANK ** -0.5),
        'w_uv': nrm((DEPTH, MLA_KV_RANK, MLA_HEADS, MLA_V), MLA_KV_RANK ** -0.5),
        'lam_qk': nrm((DEPTH, 4, DIFF_DIM), 0.1),
        'g_subln': gain((DEPTH, 2 * DIFF_DIM)),
        'b_f': FOX_BIAS_INIT + 0.1 * nrm((DEPTH, FOX_HEADS)),
        'g_grp': gain((DEPTH, 3, GROUP_W)),
        'w_out': nrm((DEPTH, D_MODEL, D_MODEL), D_MODEL ** -0.5),
        'rel_table': nrm((N_BUCKETS, DIFF_HEADS), 0.5),
        'g_ffn': gain((DEPTH, D_MODEL)),
        'w1_dense': nrm((n_dense, D_MODEL, D_FF), D_MODEL ** -0.5),
        'w3_dense': nrm((n_dense, D_MODEL, D_FF), D_MODEL ** -0.5),
        'w2_dense': nrm((n_dense, D_FF, D_MODEL), D_FF ** -0.5),
        'w_router': nrm((n_moe, D_MODEL, N_EXPERTS), D_MODEL ** -0.5),
        'w1_exp': nrm((n_moe, N_EXPERTS, D_MODEL, D_FF_EXPERT), D_MODEL ** -0.5),
        'w3_exp': nrm((n_moe, N_EXPERTS, D_MODEL, D_FF_EXPERT), D_MODEL ** -0.5),
        'w2_exp': nrm((n_moe, N_EXPERTS, D_FF_EXPERT, D_MODEL), D_FF_EXPERT ** -0.5),
        'g_final': gain((D_MODEL,)),
    }


def reference(x_prompt, x_sample, cache_sb_k, cache_sb_v, cache_mla_ckv, cache_mla_kr,
              cache_diff_k, cache_diff_v, cache_fox_k, cache_fox_v, cache_fox_logf, page_table,
              g_attn, w_in, g_cq, w_uq, g_ckv, w_uk, w_uv, lam_qk, g_subln, b_f, g_grp, w_out,
              rel_table, g_ffn, w1_dense, w3_dense, w2_dense, w_router, w1_exp, w3_exp, w2_exp,
              g_final):
    caches = (cache_sb_k, cache_sb_v, cache_mla_ckv, cache_mla_kr, cache_diff_k, cache_diff_v,
              cache_fox_k, cache_fox_v, cache_fox_logf)
    past_len = page_table.shape[1] * cache_sb_k.shape[2]
    pos_p = jnp.arange(x_prompt.shape[1], dtype=jnp.int32)
    pos_s = past_len + jnp.arange(x_sample.shape[1], dtype=jnp.int32)
    ffn_w = (w1_dense, w3_dense, w2_dense, w_router, w1_exp, w3_exp, w2_exp)
    xp, xs = x_prompt, x_sample
    rows_p, rows_s = [], []
    for l in range(DEPTH):
        mixer_w = (w_in[l], g_cq[l], w_uq[l], g_ckv[l], w_uk[l], w_uv[l], lam_qk[l], g_subln[l],
                   b_f[l], g_grp[l], w_out[l])
        past = tuple(gather_pages(c, l, page_table) for c in caches)
        mp, rp = token_mixers(rmsnorm(xp, g_attn[l]), pos_p, None, l, rel_table, *mixer_w)
        ms, rs = token_mixers(rmsnorm(xs, g_attn[l]), pos_s, past, l, rel_table, *mixer_w)
        xp = xp + mp
        xs = xs + ms
        xp = xp + channel_mixer(rmsnorm(xp, g_ffn[l]), l, *ffn_w)
        xs = xs + channel_mixer(rmsnorm(xs, g_ffn[l]), l, *ffn_w)
        rows_p.append(rp)
        rows_s.append(rs)
    y_prompt = rmsnorm(xp, g_final)
    y_sample = rmsnorm(xs, g_final)
    def stacked(rows, i):
        return jnp.stack([r[i] for r in rows])
    return (y_prompt, y_sample,
            stacked(rows_p, 0), stacked(rows_s, 0),
            stacked(rows_p, 1), stacked(rows_s, 1),
            stacked(rows_p, 2), stacked(rows_s, 2),
            stacked(rows_p, 3), stacked(rows_s, 3),
            stacked(rows_p, 4), stacked(rows_s, 4),
            stacked(rows_p, 5), stacked(rows_s, 5),
            stacked(rows_p, 6), stacked(rows_s, 6),
            stacked(rows_p, 7), stacked(rows_s, 7),
            stacked(rows_p, 8), stacked(rows_s, 8))
```

```python
import functools
import math

import numpy as np
import jax
import jax.numpy as jnp
from jax import lax
from jax.experimental import pallas as pl
from jax.experimental.pallas import tpu as pltpu

F32 = jnp.float32
BF16 = jnp.bfloat16
I32 = jnp.int32

N_HEADS = 4
HEAD_W = 64
MLA_NOPE = 64
MLA_ROPE = 32
MLA_Q_RANK = 192
MLA_KV_RANK = 128
DIFF_DIM = 32
N_BUCKETS = 32
REL_MAX_DIST = 128
ROPE_THETA = 10000.0
N_EXPERTS = 8
EPS = 1e-6
NEG_INF = -1e30
LANE = 128
SUBLANE = 8
VMEM_LIMIT = 56 * 1024 * 1024

C_QA, C_G2, C_QC, C_QD, C_KAVA, C_CKV, C_KCVC, C_KDVD, C_END = 0, 256, 512, 768, 1024, 1152, 1280, 1408, 1536
ROPE_LO, ROPE_MID, ROPE_HI = 64, 80, 96
LOGF_LO, LOGF_HI = 96, 100


def _cparams(sem):
    return pltpu.CompilerParams(dimension_semantics=sem, vmem_limit_bytes=VMEM_LIMIT)


def _dot(a, b):
    return jnp.dot(a, b, preferred_element_type=F32)


def _dot_nt(a, b):
    return lax.dot_general(a, b, (((1,), (1,)), ((), ())), preferred_element_type=F32)


def _rms(x, g):
    ms = jnp.mean(x * x, axis=-1, keepdims=True)
    return x * lax.rsqrt(ms + EPS) * g


def _log_sigmoid(z):
    return jnp.minimum(z, 0.0) - jnp.log1p(jnp.exp(-jnp.abs(z)))


def _split2(x):
    hi = x.astype(BF16)
    lo = (x - hi.astype(F32)).astype(BF16)
    return hi, lo


def _split3(x):
    hi = x.astype(BF16)
    r = x - hi.astype(F32)
    mid = r.astype(BF16)
    lo = (r - mid.astype(F32)).astype(BF16)
    return hi, mid, lo


def _lane_iota(shape):
    return lax.broadcasted_iota(I32, shape, len(shape) - 1)


def _proj_body(x_ref, g_ref, w_ref, o_ref):
    h = _rms(x_ref[...], g_ref[...])
    o_ref[...] = _dot(h.astype(BF16), w_ref[...])


def _proj(x, g, w, tm):
    n, d = x.shape
    return pl.pallas_call(
        _proj_body,
        grid=(n // tm,),
        in_specs=[pl.BlockSpec((tm, d), lambda i: (i, 0)),
                  pl.BlockSpec((1, d), lambda i: (0, 0)),
                  pl.BlockSpec((d, C_END), lambda i: (0, 0))],
        out_specs=pl.BlockSpec((tm, C_END), lambda i: (i, 0)),
        out_shape=jax.ShapeDtypeStruct((n, C_END), F32),
        compiler_params=_cparams(("parallel",)),
        name="proj",
    )(x, g, w)


def _rope_tile(t, c, s):
    lane = _lane_iota(t.shape)
    first = (lane >= ROPE_LO) & (lane < ROPE_MID)
    rot = jnp.where(first, pltpu.roll(t, LANE - 16, 1), pltpu.roll(t, 16, 1))
    return t * c + rot * s


def _prep_body(p2_ref, ckv_ref, gcq_ref, gckv_ref, wuq_ref, wuk_ref, c_ref, s_ref, bf_ref, qmla_ref, kmla_ref):
    g2 = p2_ref[...]
    lane = _lane_iota(g2.shape)
    cq = jnp.where(lane < MLA_Q_RANK, g2, 0.0)
    ms = jnp.sum(cq * cq, axis=-1, keepdims=True) * (1.0 / MLA_Q_RANK)
    cqn = cq * lax.rsqrt(ms + EPS) * gcq_ref[...]
    qb = _dot(cqn.astype(BF16), wuq_ref[...])
    qlat = _dot(qb[:, :N_HEADS * MLA_NOPE].astype(BF16), wuk_ref[...])
    c = c_ref[...]
    s = s_ref[...]
    for h in range(N_HEADS):
        qr = _rope_tile(qb[:, 256 + h * LANE:256 + (h + 1) * LANE], c, s)
        qmla_ref[:, h * 256:h * 256 + LANE] = qlat[:, h * LANE:(h + 1) * LANE].astype(BF16)
        qmla_ref[:, h * 256 + LANE:(h + 1) * 256] = qr.astype(BF16)
    t2 = g2[:, LANE:]
    lane2 = _lane_iota(t2.shape)
    logf = jnp.where((lane2 >= LOGF_LO) & (lane2 < LOGF_HI), _log_sigmoid(t2 + bf_ref[...]), 0.0)
    kmla_ref[:, :LANE] = _rms(ckv_ref[...], gckv_ref[...])
    kmla_ref[:, LANE:] = _rope_tile(t2, c, s) + logf


def _prep(p, gcq, gckv, wuq, wuk, tab_c, tab_s, bfp, tm):
    n = p.shape[0]
    nrep = tab_c.shape[0] // tm
    return pl.pallas_call(
        _prep_body,
        grid=(n // tm,),
        in_specs=[pl.BlockSpec((tm, 256), lambda i: (i, C_G2 // 256)),
                  pl.BlockSpec((tm, LANE), lambda i: (i, C_CKV // LANE)),
                  pl.BlockSpec((1, 256), lambda i: (0, 0)),
                  pl.BlockSpec((1, LANE), lambda i: (0, 0)),
                  pl.BlockSpec(wuq.shape, lambda i: (0, 0)),
                  pl.BlockSpec(wuk.shape, lambda i: (0, 0)),
                  pl.BlockSpec((tm, LANE), lambda i: (i % nrep, 0)),
                  pl.BlockSpec((tm, LANE), lambda i: (i % nrep, 0)),
                  pl.BlockSpec((1, LANE), lambda i: (0, 0))],
        out_specs=[pl.BlockSpec((tm, 1024), lambda i: (i, 0)),
                   pl.BlockSpec((tm, 256), lambda i: (i, 0))],
        out_shape=[jax.ShapeDtypeStruct((n, 1024), BF16), jax.ShapeDtypeStruct((n, 256), F32)],
        compiler_params=_cparams(("parallel",)),
        name="prep",
    )(p, p, gcq, gckv, wuq, wuk, tab_c, tab_s, bfp)


def _cum_body(t_ref, l_ref, ccol_ref, crow_ref, *, tq):
    x = t_ref[...]
    lane = _lane_iota(x.shape)
    x = jnp.where((lane >= LOGF_LO) & (lane < LOGF_HI), x, 0.0)
    hi, mid, lo = _split3(x)
    ltri = l_ref[...]
    c = _dot(ltri, hi) + _dot(ltri, mid) + _dot(ltri, lo)
    ccol_ref[...] = c
    ct = c.T
    for j in range(x.shape[0] // tq):
        crow_ref[0, j] = ct[LOGF_LO:LOGF_LO + SUBLANE, j * tq:(j + 1) * tq]


def _cum(kmla, ltri, bsz, t_len, tq):
    nq = t_len // tq
    return pl.pallas_call(
        functools.partial(_cum_body, tq=tq),
        grid=(bsz,),
        in_specs=[pl.BlockSpec((t_len, LANE), lambda b: (b, 1)),
                  pl.BlockSpec((t_len, t_len), lambda b: (0, 0))],
        out_specs=[pl.BlockSpec((t_len, LANE), lambda b: (b, 0)),
                   pl.BlockSpec((1, nq, SUBLANE, tq), lambda b: (b, 0, 0, 0))],
        out_shape=[jax.ShapeDtypeStruct((bsz * t_len, LANE), F32),
                   jax.ShapeDtypeStruct((bsz, nq, SUBLANE, tq), F32)],
        compiler_params=_cparams(("parallel",)),
        name="logf_cumsum",
    )(kmla, ltri)


def _stack_heads(q, lo, width):
    tq = q.shape[0]
    lane = _lane_iota((tq, LANE))
    keep = (lane >= lo) & (lane < lo + width)
    parts = []
    for h in range(N_HEADS):
        t = q[:, (h // 2) * LANE:(h // 2 + 1) * LANE]
        if h % 2:
            t = pltpu.roll(t, HEAD_W, 1)
        parts.append(jnp.where(keep, t, 0.0))
    return jnp.concatenate(parts, axis=0)


def _unstack_heads(acc):
    tq = acc.shape[0] // N_HEADS
    lane = _lane_iota((tq, LANE))
    tiles = []
    for a in range(2):
        even = acc[(2 * a) * tq:(2 * a + 1) * tq]
        odd = acc[(2 * a + 1) * tq:(2 * a + 2) * tq]
        tiles.append(jnp.where(lane < HEAD_W, pltpu.roll(even, HEAD_W, 1), odd))
    return jnp.concatenate(tiles, axis=1)


def _diag_masks(m, tq):
    row = lax.broadcasted_iota(I32, (m, tq), 0) % tq
    col = lax.broadcasted_iota(I32, (m, tq), 1)
    return row, col


def _online(s, m_ref, l_ref):
    m_prev = m_ref[...]
    m_new = jnp.maximum(m_prev, jnp.max(s, axis=1, keepdims=True))
    alpha = jnp.exp(m_prev - m_new)
    p = jnp.exp(s - m_new)
    l_ref[...] = alpha * l_ref[...] + jnp.sum(p, axis=1, keepdims=True)
    m_ref[...] = m_new
    return p, alpha


def _sb_body(q_ref, kv_ref, o_ref, acc_ref, *, tq):
    qi = pl.program_id(1)
    m = N_HEADS * tq
    qs = _stack_heads(q_ref[...], 0, HEAD_W).astype(BF16)
    r = lax.broadcasted_iota(I32, (tq, tq), 0)
    c = lax.broadcasted_iota(I32, (tq, tq), 1)
    upper = (r > c).astype(BF16)
    row, col = _diag_masks(m, tq)
    dmask = col < row
    acc_ref[...] = jnp.zeros_like(acc_ref)

    def step(j, carry, masked):
        kvb = kv_ref[pl.ds(pl.multiple_of(j * tq, tq), tq), :].astype(BF16)
        z = _dot_nt(qs, kvb) * (HEAD_W ** -0.5)
        lb = _log_sigmoid(z)
        l1m = lb - z
        if masked:
            l1m = jnp.where(dmask, l1m, 0.0)
        hi, lo = _split2(l1m)
        after = _dot(hi, upper) + _dot(lo, upper) + carry
        w = jnp.exp(lb + after)
        if masked:
            w = jnp.where(dmask, w, 0.0)
        acc_ref[...] += _dot(w.astype(BF16), kvb)
        return carry + jnp.sum(l1m, axis=1, keepdims=True)

    carry = step(qi, jnp.zeros((m, 1), F32), True)
    lax.fori_loop(0, qi, lambda jj, cr: step(qi - 1 - jj, cr, False), carry)
    o_ref[...] = _unstack_heads(acc_ref[...])


def _sb_attn(p, bsz, t_len, tq):
    nq = t_len // tq
    return pl.pallas_call(
        functools.partial(_sb_body, tq=tq),
        grid=(bsz, nq),
        in_specs=[pl.BlockSpec((tq, 256), lambda b, i: (b * nq + i, C_QA // 256)),
                  pl.BlockSpec((t_len, LANE), lambda b, i: (b, C_KAVA // LANE))],
        out_specs=pl.BlockSpec((tq, 256), lambda b, i: (b * nq + i, 0)),
        out_shape=jax.ShapeDtypeStruct((bsz * t_len, 256), F32),
        scratch_shapes=[pltpu.VMEM((N_HEADS * tq, LANE), F32)],
        compiler_params=_cparams(("parallel", "parallel")),
        name="sb_attn",
    )(p, p)


def _mla_body(q_ref, k_ref, wuv_ref, o_ref, acc_ref, m_ref, l_ref, *, tq):
    qi = pl.program_id(1)
    m = N_HEADS * tq
    qs = jnp.concatenate([q_ref[:, h * 256:(h + 1) * 256] for h in range(N_HEADS)], axis=0)
    row, col = _diag_masks(m, tq)
    dmask = col <= row
    acc_ref[...] = jnp.zeros_like(acc_ref)
    m_ref[...] = jnp.full_like(m_ref, NEG_INF)
    l_ref[...] = jnp.zeros_like(l_ref)
    scale = (MLA_NOPE + MLA_ROPE) ** -0.5

    def step(j, masked):
        kb = k_ref[pl.ds(pl.multiple_of(j * tq, tq), tq), :].astype(BF16)
        s = _dot_nt(qs, kb) * scale
        if masked:
            s = jnp.where(dmask, s, NEG_INF)
        p, alpha = _online(s, m_ref, l_ref)
        acc_ref[...] = alpha * acc_ref[...] + _dot(p.astype(BF16), kb)

    step(qi, True)

    def loop(jj, _):
        step(qi - 1 - jj, False)
        return 0

    lax.fori_loop(0, qi, loop, 0)
    o = acc_ref[...] / l_ref[...]
    out = jnp.zeros((tq, 256), F32)
    for h in range(N_HEADS):
        out = out + _dot(o[h * tq:(h + 1) * tq, :MLA_KV_RANK].astype(BF16), wuv_ref[h])
    o_ref[...] = out


def _mla_attn(qmla, kmla, wuv, bsz, t_len, tq):
    nq = t_len // tq
    m = N_HEADS * tq
    return pl.pallas_call(
        functools.partial(_mla_body, tq=tq),
        grid=(bsz, nq),
        in_specs=[pl.BlockSpec((tq, 1024), lambda b, i: (b * nq + i, 0)),
                  pl.BlockSpec((t_len, 256), lambda b, i: (b, 0)),
                  pl.BlockSpec(wuv.shape, lambda b, i: (0, 0, 0))],
        out_specs=pl.BlockSpec((tq, 256), lambda b, i: (b * nq + i, 0)),
        out_shape=jax.ShapeDtypeStruct((bsz * t_len, 256), F32),
        scratch_shapes=[pltpu.VMEM((m, 256), F32), pltpu.VMEM((m, 1), F32), pltpu.VMEM((m, 1), F32)],
        compiler_params=_cparams(("parallel", "parallel")),
        name="mla_attn",
    )(qmla, kmla, wuv)


def _lambda(lam_ref, lam_init):
    lq = lam_ref[...]
    a = jnp.sum(lq[0:1] * lq[1:2], axis=1, keepdims=True)
    b = jnp.sum(lq[2:3] * lq[3:4], axis=1, keepdims=True)
    return jnp.exp(a) - jnp.exp(b) + lam_init


def _diff_body(lam_ref, q_ref, kv_ref, bias_ref, bfar_ref, o_ref,
               acc0, acc1, m0, m1, l0, l1, *, tq, lam_init):
    qi = pl.program_id(1)
    m = N_HEADS * tq
    q = q_ref[...]
    qmaps = [_stack_heads(q, i * DIFF_DIM, DIFF_DIM).astype(BF16) for i in range(2)]
    row, col = _diag_masks(m, tq)
    dmask = col <= row
    accs, ms, ls = (acc0, acc1), (m0, m1), (l0, l1)
    for i in range(2):
        accs[i][...] = jnp.zeros_like(accs[i])
        ms[i][...] = jnp.full_like(ms[i], NEG_INF)
        ls[i][...] = jnp.zeros_like(ls[i])
    scale = DIFF_DIM ** -0.5

    def step(j, kind):
        kvb = kv_ref[pl.ds(pl.multiple_of(j * tq, tq), tq), :].astype(BF16)
        if kind == 0:
            bias = bias_ref[0]
        elif kind == 1:
            bias = bias_ref[1]
        else:
            bias = bfar_ref[...]
        for i in range(2):
            s = _dot_nt(qmaps[i], kvb) * scale + bias
            if kind == 0:
                s = jnp.where(dmask, s, NEG_INF)
            p, alpha = _online(s, ms[i], ls[i])
            accs[i][...] = alpha * accs[i][...] + _dot(p.astype(BF16), kvb)

    step(qi, 0)

    @pl.when(qi >= 1)
    def _():
        step(qi - 1, 1)

    def loop(jj, _):
        step(qi - 2 - jj, 2)
        return 0

    lax.fori_loop(0, jnp.maximum(qi - 1, 0), loop, 0)
    lam = _lambda(lam_ref, lam_init)
    o = acc0[...] / l0[...] - lam * (acc1[...] / l1[...])
    o_ref[...] = _unstack_heads(o)


def _diff_attn(lam_qk, p, bias01, bfar, bsz, t_len, tq, lam_init):
    nq = t_len // tq
    m = N_HEADS * tq
    col = pltpu.VMEM((m, 1), F32)
    return pl.pallas_call(
        functools.partial(_diff_body, tq=tq, lam_init=lam_init),
        grid=(bsz, nq),
        in_specs=[pl.BlockSpec(lam_qk.shape, lambda b, i: (0, 0)),
                  pl.BlockSpec((tq, 256), lambda b, i: (b * nq + i, C_QC // 256)),
                  pl.BlockSpec((t_len, LANE), lambda b, i: (b, C_KCVC // LANE)),
                  pl.BlockSpec(bias01.shape, lambda b, i: (0, 0, 0)),
                  pl.BlockSpec(bfar.shape, lambda b, i: (0, 0))],
        out_specs=pl.BlockSpec((tq, 256), lambda b, i: (b * nq + i, 0)),
        out_shape=jax.ShapeDtypeStruct((bsz * t_len, 256), F32),
        scratch_shapes=[pltpu.VMEM((m, LANE), F32), pltpu.VMEM((m, LANE), F32), col, col, col, col],
        compiler_params=_cparams(("parallel", "parallel")),
        name="diff_attn",
    )(lam_qk, p, p, bias01, bfar)


def _fox_body(q_ref, kv_ref, ccol_ref, crow_ref, o_ref, acc_ref, m_ref, l_ref, *, tq):
    qi = pl.program_id(1)
    m = N_HEADS * tq
    qs = _stack_heads(q_ref[...], 0, HEAD_W).astype(BF16)
    cc = ccol_ref[...]
    cq = jnp.concatenate([cc[:, LOGF_LO + h:LOGF_LO + h + 1] for h in range(N_HEADS)], axis=0)
    row, col = _diag_masks(m, tq)
    dmask = col <= row
    acc_ref[...] = jnp.zeros_like(acc_ref)
    m_ref[...] = jnp.full_like(m_ref, NEG_INF)
    l_ref[...] = jnp.zeros_like(l_ref)

    def step(j, masked):
        kvb = kv_ref[pl.ds(pl.multiple_of(j * tq, tq), tq), :].astype(BF16)
        ck = crow_ref[0, j]
        s = _dot_nt(qs, kvb) * (HEAD_W ** -0.5) + cq
        s = jnp.concatenate([s[h * tq:(h + 1) * tq] - ck[h:h + 1] for h in range(N_HEADS)], axis=0)
        if masked:
            s = jnp.where(dmask, s, NEG_INF)
        p, alpha = _online(s, m_ref, l_ref)
        acc_ref[...] = alpha * acc_ref[...] + _dot(p.astype(BF16), kvb)

    step(qi, True)

    def loop(jj, _):
        step(qi - 1 - jj, False)
        return 0

    lax.fori_loop(0, qi, loop, 0)
    o_ref[...] = _unstack_heads(acc_ref[...] / l_ref[...])


def _fox_attn(p, ccol, crow, bsz, t_len, tq):
    nq = t_len // tq
    m = N_HEADS * tq
    return pl.pallas_call(
        functools.partial(_fox_body, tq=tq),
        grid=(bsz, nq),
        in_specs=[pl.BlockSpec((tq, 256), lambda b, i: (b * nq + i, C_QD // 256)),
                  pl.BlockSpec((t_len, LANE), lambda b, i: (b, C_KDVD // LANE)),
                  pl.BlockSpec((tq, LANE), lambda b, i: (b * nq + i, 0)),
                  pl.BlockSpec((1, nq, SUBLANE, tq), lambda b, i: (b, 0, 0, 0))],
        out_specs=pl.BlockSpec((tq, 256), lambda b, i: (b * nq + i, 0)),
        out_shape=jax.ShapeDtypeStruct((bsz * t_len, 256), F32),
        scratch_shapes=[pltpu.VMEM((m, LANE), F32), pltpu.VMEM((m, 1), F32), pltpu.VMEM((m, 1), F32)],
        compiler_params=_cparams(("parallel", "parallel")),
        name="fox_attn",
    )(p, p, ccol, crow)


def _post_body(oa_ref, ob_ref, oc_ref, od_ref, x_ref, ggrp_ref, gsub_ref, w_ref, o_ref):
    gg = ggrp_ref[...]
    a = _rms(oa_ref[...], gg[0:1])
    b = _rms(ob_ref[...], gg[1:2])
    d = _rms(od_ref[...], gg[2:3])
    oc = oc_ref[...]
    lane = _lane_iota(oc.shape)
    sq = oc * oc
    ms = jnp.zeros_like(oc)
    for h in range(N_HEADS):
        mask = (lane >= h * HEAD_W) & (lane < (h + 1) * HEAD_W)
        sh = jnp.sum(jnp.where(mask, sq, 0.0), axis=1, keepdims=True) * (1.0 / HEAD_W)
        ms = jnp.where(mask, sh, ms)
    c = oc * lax.rsqrt(ms + EPS) * gsub_ref[...]
    y = (_dot(a.astype(BF16), w_ref[0]) + _dot(b.astype(BF16), w_ref[1])
         + _dot(c.astype(BF16), w_ref[2]) + _dot(d.astype(BF16), w_ref[3]))
    o_ref[...] = x_ref[...] + y


def _post(oa, ob, oc, od, x, ggrp, gsub, w4, tm):
    n, d = x.shape
    blk = pl.BlockSpec((tm, 256), lambda i: (i, 0))
    return pl.pallas_call(
        _post_body,
        grid=(n // tm,),
        in_specs=[blk, blk, blk, blk,
                  pl.BlockSpec((tm, d), lambda i: (i, 0)),
                  pl.BlockSpec(ggrp.shape, lambda i: (0, 0)),
                  pl.BlockSpec((1, 256), lambda i: (0, 0)),
                  pl.BlockSpec(w4.shape, lambda i: (0, 0, 0))],
        out_specs=pl.BlockSpec((tm, d), lambda i: (i, 0)),
        out_shape=jax.ShapeDtypeStruct((n, d), F32),
        compiler_params=_cparams(("parallel",)),
        name="post",
    )(oa, ob, oc, od, x, ggrp, gsub, w4)


def _ffn_body(x_ref, g_ref, w1_ref, w3_ref, w2_ref, o_ref, h_ref, acc_ref):
    j = pl.program_id(1)

    @pl.when(j == 0)
    def _():
        h_ref[...] = _rms(x_ref[...], g_ref[...]).astype(BF16)
        acc_ref[...] = jnp.zeros_like(acc_ref)

    h = h_ref[...]
    a = _dot(h, w1_ref[...])
    b = _dot(h, w3_ref[...])
    t = a * jax.nn.sigmoid(a) * b
    acc_ref[...] += _dot(t.astype(BF16), w2_ref[...])

    @pl.when(j == pl.num_programs(1) - 1)
    def _():
        o_ref[...] = x_ref[...] + acc_ref[...]


def _ffn(x, g, w1, w3, w2, tm, tf):
    n, d = x.shape
    dff = w1.shape[1]
    return pl.pallas_call(
        _ffn_body,
        grid=(n // tm, dff // tf),
        in_specs=[pl.BlockSpec((tm, d), lambda i, j: (i, 0)),
                  pl.BlockSpec((1, d), lambda i, j: (0, 0)),
                  pl.BlockSpec((d, tf), lambda i, j: (0, j)),
                  pl.BlockSpec((d, tf), lambda i, j: (0, j)),
                  pl.BlockSpec((tf, d), lambda i, j: (j, 0))],
        out_specs=pl.BlockSpec((tm, d), lambda i, j: (i, 0)),
        out_shape=jax.ShapeDtypeStruct((n, d), F32),
        scratch_shapes=[pltpu.VMEM((tm, d), BF16), pltpu.VMEM((tm, d), F32)],
        compiler_params=_cparams(("parallel", "arbitrary")),
        name="ffn_dense",
    )(x, g, w1, w3, w2)


def _router_body(x_ref, g_ref, w_ref, o_ref):
    h = _rms(x_ref[...], g_ref[...])
    h1, h2, h3 = _split3(h)
    w1, w2, w3 = w_ref[0], w_ref[1], w_ref[2]
    logits = (_dot(h1, w1) + _dot(h1, w2) + _dot(h2, w1)
              + _dot(h2, w2) + _dot(h1, w3) + _dot(h3, w1))
    lane = _lane_iota(logits.shape)
    neg = -jnp.inf
    l1 = jnp.where(lane < N_EXPERTS, logits, neg)
    m1 = jnp.max(l1, axis=1, keepdims=True)
    i1 = jnp.min(jnp.where(l1 == m1, lane, LANE), axis=1, keepdims=True)
    l2 = jnp.where(lane == i1, neg, l1)
    m2 = jnp.max(l2, axis=1, keepdims=True)
    i2 = jnp.min(jnp.where(l2 == m2, lane, LANE), axis=1, keepdims=True)
    e = jnp.exp(m2 - m1)
    den = 1.0 + e
    g1 = 1.0 / den
    g2 = e / den
    out = jnp.where(lane == 0, i1.astype(F32),
                    jnp.where(lane == 1, i2.astype(F32),
                              jnp.where(lane == 2, g1, jnp.where(lane == 3, g2, 0.0))))
    o_ref[...] = out


def _router(x, g, w3, tm):
    n, d = x.shape
    return pl.pallas_call(
        _router_body,
        grid=(n // tm,),
        in_specs=[pl.BlockSpec((tm, d), lambda i: (i, 0)),
                  pl.BlockSpec((1, d), lambda i: (0, 0)),
                  pl.BlockSpec(w3.shape, lambda i: (0, 0, 0))],
        out_specs=pl.BlockSpec((tm, LANE), lambda i: (i, 0)),
        out_shape=jax.ShapeDtypeStruct((n, LANE), F32),
        compiler_params=_cparams(("parallel",)),
        name="router",
    )(x, g, w3)


def _row_copy(src_hbm, idx, dst, r, sem):
    return pltpu.make_async_copy(src_hbm.at[pl.ds(idx, 1)], dst.at[pl.ds(r, 1)], sem)


def _moe_body(be_ref, nv_ref, tok_hbm, x_hbm, gate_ref, g_ref, w1_ref, w3_ref, w2_ref, o_ref,
              tok_smem, xbuf, hbuf, acc_ref, sems, *, rb):
    i = pl.program_id(0)
    j = pl.program_id(1)
    nj = pl.num_programs(1)
    live = i < nv_ref[0]

    @pl.when(live & (j == 0))
    def _():
        cp = pltpu.make_async_copy(tok_hbm.at[i], tok_smem, sems.at[0])
        cp.start()
        cp.wait()

        def issue(r, _):
            _row_copy(x_hbm, tok_smem[r], xbuf, r, sems.at[1]).start()
            return 0

        lax.fori_loop(0, rb, issue, 0)

        def drain(r, _):
            _row_copy(x_hbm, 0, xbuf, r, sems.at[1]).wait()
            return 0

        lax.fori_loop(0, rb, drain, 0)
        hbuf[...] = _rms(xbuf[...], g_ref[...]).astype(BF16)
        acc_ref[...] = jnp.zeros_like(acc_ref)

    @pl.when(live)
    def _():
        h = hbuf[...]
        a = _dot(h, w1_ref[...])
        b = _dot(h, w3_ref[...])
        t = a * jax.nn.sigmoid(a) * b
        acc_ref[...] += _dot(t.astype(BF16), w2_ref[...])

    @pl.when(live & (j == nj - 1))
    def _():
        o_ref[...] = acc_ref[...] * gate_ref[...]

    @pl.when(jnp.logical_not(live) & (j == nj - 1))
    def _():
        o_ref[...] = jnp.zeros_like(o_ref)


def _moe_experts(block_e, nvalid, row_tok, x, row_gate, g, w1, w3, w2, rb, tf):
    nb = row_tok.shape[0]
    n, d = x.shape
    dff = w1.shape[2]
    nj = dff // tf

    def jeff(i, j, nv):
        return jnp.where(i < nv[0], j, nj - 1)

    grid_spec = pltpu.PrefetchScalarGridSpec(
        num_scalar_prefetch=2,
        grid=(nb, nj),
        in_specs=[pl.BlockSpec(memory_space=pl.ANY),
                  pl.BlockSpec(memory_space=pl.ANY),
                  pl.BlockSpec((rb, 1), lambda i, j, be, nv: (i, 0)),
                  pl.BlockSpec((1, d), lambda i, j, be, nv: (0, 0)),
                  pl.BlockSpec((None, d, tf), lambda i, j, be, nv: (be[i], 0, jeff(i, j, nv))),
                  pl.BlockSpec((None, d, tf), lambda i, j, be, nv: (be[i], 0, jeff(i, j, nv))),
                  pl.BlockSpec((None, tf, d), lambda i, j, be, nv: (be[i], jeff(i, j, nv), 0))],
        out_specs=pl.BlockSpec((rb, d), lambda i, j, be, nv: (i, 0)),
        scratch_shapes=[pltpu.SMEM((rb,), I32), pltpu.VMEM((rb, d), F32), pltpu.VMEM((rb, d), BF16),
                        pltpu.VMEM((rb, d), F32), pltpu.SemaphoreType.DMA((2,))],
    )
    return pl.pallas_call(
        functools.partial(_moe_body, rb=rb),
        grid_spec=grid_spec,
        out_shape=jax.ShapeDtypeStruct((nb * rb, d), F32),
        compiler_params=_cparams(("arbitrary", "arbitrary")),
        name="moe_experts",
    )(block_e, nvalid, row_tok, x, row_gate, g, w1, w3, w2)


def _comb_body(pos_hbm, y_hbm, x_ref, o_ref, pos_smem, ybuf, sems, *, tm):
    i = pl.program_id(0)
    cp = pltpu.make_async_copy(pos_hbm.at[i], pos_smem, sems.at[0])
    cp.start()
    cp.wait()

    def issue(r, _):
        _row_copy(y_hbm, pos_smem[r], ybuf, r, sems.at[1]).start()
        return 0

    lax.fori_loop(0, 2 * tm, issue, 0)

    def drain(r, _):
        _row_copy(y_hbm, 0, ybuf, r, sems.at[1]).wait()
        return 0

    lax.fori_loop(0, 2 * tm, drain, 0)
    o_ref[...] = x_ref[...] + ybuf[:tm] + ybuf[tm:]


def _moe_combine(pos, y, x, tm):
    n, d = x.shape
    return pl.pallas_call(
        functools.partial(_comb_body, tm=tm),
        grid=(n // tm,),
        in_specs=[pl.BlockSpec(memory_space=pl.ANY),
                  pl.BlockSpec(memory_space=pl.ANY),
                  pl.BlockSpec((tm, d), lambda i: (i, 0))],
        out_specs=pl.BlockSpec((tm, d), lambda i: (i, 0)),
        out_shape=jax.ShapeDtypeStruct((n, d), F32),
        scratch_shapes=[pltpu.SMEM((2 * tm,), I32), pltpu.VMEM((2 * tm, d), F32), pltpu.SemaphoreType.DMA((2,))],
        compiler_params=_cparams(("arbitrary",)),
        name="moe_combine",
    )(pos, y, x)


def _moe(x, g, wr3, w1, w3, w2, tm, rb, tf, tc):
    n, d = x.shape
    r = _router(x, g, wr3, tm)
    flat_e = r[:, 0:2].astype(I32).reshape(-1)
    flat_g = r[:, 2:4].reshape(-1)
    n_assign = 2 * n
    onehot = (flat_e[:, None] == jnp.arange(N_EXPERTS, dtype=I32)[None, :]).astype(I32)
    csum = jnp.cumsum(onehot, axis=0)
    rank = jnp.take_along_axis(csum, flat_e[:, None], axis=1)[:, 0] - 1
    counts = csum[-1]
    padded = (counts + rb - 1) // rb * rb
    pend = jnp.cumsum(padded)
    pstart = pend - padded
    dest = pstart[flat_e] + rank
    nb = -(-n_assign // rb) + N_EXPERTS
    n_rows = nb * rb
    row_tok = jnp.zeros((n_rows,), I32).at[dest].set(jnp.arange(n_assign, dtype=I32) // 2)
    row_gate = jnp.zeros((n_rows,), F32).at[dest].set(flat_g)
    block_start = jnp.arange(nb, dtype=I32) * rb
    block_e = jnp.minimum(jnp.searchsorted(pend, block_start, side="right"), N_EXPERTS - 1).astype(I32)
    nvalid = (pend[-1:] // rb).astype(I32)
    y = _moe_experts(block_e, nvalid, row_tok.reshape(nb, rb), x, row_gate.reshape(n_rows, 1), g, w1, w3, w2, rb, tf)
    pos = dest.reshape(n // tc, tc, 2).transpose(0, 2, 1).reshape(n // tc, 2 * tc)
    return _moe_combine(pos, y, x, tc)


def _norm_body(x_ref, g_ref, o_ref):
    o_ref[...] = _rms(x_ref[...], g_ref[...])


def _final_norm(x, g, tm):
    n, d = x.shape
    return pl.pallas_call(
        _norm_body,
        grid=(n // tm,),
        in_specs=[pl.BlockSpec((tm, d), lambda i: (i, 0)), pl.BlockSpec((1, d), lambda i: (0, 0))],
        out_specs=pl.BlockSpec((tm, d), lambda i: (i, 0)),
        out_shape=jax.ShapeDtypeStruct((n, d), F32),
        compiler_params=_cparams(("parallel",)),
        name="final_norm",
    )(x, g)


def _dec_body(pt_ref,
              qa_ref, ql_ref, qr_ref, qd_ref, qf_ref,
              ckvn_ref, krn_ref, kcn_ref, vcn_ref, kdn_ref, vdn_ref, lfn_ref,
              brel_ref, bnew_ref, lam_ref, wuv_ref,
              sbk_h, sbv_h, ckv_h, kr_h, dk_h, dv_h, fk_h, fv_h, lf_h,
              oa_ref, ob_ref, oc_ref, od_ref,
              b_sbk, b_sbv, b_ckv, b_kr, b_dk, b_dv, b_fk, b_fv, b_lf, sems,
              acc_a, car_a, m_b, l_b, acc_b, m_c, l_c, acc_c, m_d, l_d, acc_d, car_d,
              *, layer, gp, n_pages, nc, total, lam_init):
    i = pl.program_id(0)
    slot = i % 2
    pairs = ((sbk_h, b_sbk), (sbv_h, b_sbv), (ckv_h, b_ckv), (kr_h, b_kr),
             (dk_h, b_dk), (dv_h, b_dv), (fk_h, b_fk), (fv_h, b_fv))

    def copies(step, sl):
        b = step // nc
        cc = nc - 1 - step % nc
        out = []
        for g in range(gp):
            phys = pt_ref[b * n_pages + cc * gp + g]
            for src, dst in pairs:
                out.append(pltpu.make_async_copy(src.at[layer, phys], dst.at[sl, g], sems.at[sl]))
            out.append(pltpu.make_async_copy(lf_h.at[layer, phys], b_lf.at[sl, g, pl.ds(0, N_HEADS)], sems.at[sl]))
        return out

    @pl.when(i == 0)
    def _():
        b_lf[...] = jnp.zeros_like(b_lf)
        for cp in copies(0, 0):
            cp.start()

    @pl.when(i + 1 < total)
    def _():
        for cp in copies(i + 1, 1 - slot):
            cp.start()

    for cp in copies(i, slot):
        cp.wait()

    c = i % nc
    qa, ql, qr, qd, qf = qa_ref[...], ql_ref[...], qr_ref[...], qd_ref[...], qf_ref[...]
    sc_sb = HEAD_W ** -0.5
    sc_mla = (MLA_NOPE + MLA_ROPE) ** -0.5
    sc_diff = DIFF_DIM ** -0.5

    def bf_row(ref):
        return ref[...].astype(BF16).astype(F32)

    @pl.when(c == 0)
    def _():
        acc_a[...] = jnp.zeros_like(acc_a)
        car_a[...] = jnp.zeros_like(car_a)
        ckvn, krn = bf_row(ckvn_ref), bf_row(krn_ref)
        s_b = (jnp.sum(ql.astype(F32) * ckvn, axis=1, keepdims=True)
               + jnp.sum(qr.astype(F32) * krn, axis=1, keepdims=True)) * sc_mla
        m_b[...] = s_b
        l_b[...] = jnp.ones_like(l_b)
        acc_b[...] = jnp.broadcast_to(ckvn, acc_b.shape)
        s_c = jnp.sum(qd.astype(F32) * bf_row(kcn_ref), axis=1, keepdims=True) * sc_diff + bnew_ref[...]
        m_c[...] = s_c
        l_c[...] = jnp.ones_like(l_c)
        acc_c[...] = jnp.broadcast_to(bf_row(vcn_ref), acc_c.shape)
        s_d = jnp.sum(qf.astype(F32) * bf_row(kdn_ref), axis=1, keepdims=True) * sc_sb
        m_d[...] = s_d
        l_d[...] = jnp.ones_like(l_d)
        acc_d[...] = jnp.broadcast_to(bf_row(vdn_ref), acc_d.shape)
        car_d[...] = lfn_ref[...]

    r = lax.broadcasted_iota(I32, (LANE, LANE), 0)
    cl = lax.broadcasted_iota(I32, (LANE, LANE), 1)
    upper = (r > cl).astype(BF16)

    def page(buf, g):
        return buf[slot, g].astype(BF16)

    def suffix_offsets(car_ref, psum):
        run = car_ref[...]
        offs = [None] * gp
        for g in reversed(range(gp)):
            offs[g] = run
            run = run + psum[g]
        car_ref[...] = run
        return jnp.stack(offs)

    def softmax_update(s, m_ref, l_ref, acc_ref, vbuf, v_rows_are_keys=False):
        mx = jnp.max(jnp.max(s, axis=2, keepdims=True), axis=0)
        m_new = jnp.maximum(m_ref[...], mx)
        alpha = jnp.exp(m_ref[...] - m_new)
        p = jnp.exp(s - m_new[None])
        l_ref[...] = alpha * l_ref[...] + jnp.sum(jnp.sum(p, axis=2, keepdims=True), axis=0)
        m_ref[...] = m_new
        acc = alpha * acc_ref[...]
        pb = p.astype(BF16)
        for g in range(gp):
            v = page(vbuf, g)
            acc = acc + (_dot(pb[g], v) if v_rows_are_keys else _dot_nt(pb[g], v))
        acc_ref[...] = acc

    z = jnp.stack([_dot(qa, page(b_sbk, g)) for g in range(gp)]) * sc_sb
    lb = _log_sigmoid(z)
    l1m = lb - z
    hi, lo = _split2(l1m.reshape(gp * SUBLANE, LANE))
    within = (_dot(hi, upper) + _dot(lo, upper)).reshape(gp, SUBLANE, LANE)
    off = suffix_offsets(car_a, jnp.sum(l1m, axis=2, keepdims=True))
    w = jnp.exp(lb + within + off).astype(BF16)
    acc = acc_a[...]
    for g in range(gp):
        acc = acc + _dot_nt(w[g], page(b_sbv, g))
    acc_a[...] = acc

    s = jnp.stack([_dot_nt(ql, page(b_ckv, g)) + _dot(qr, page(b_kr, g)) for g in range(gp)]) * sc_mla
    softmax_update(s, m_b, l_b, acc_b, b_ckv, v_rows_are_keys=True)

    cc = nc - 1 - c
    s = jnp.stack([_dot(qd, page(b_dk, g)) * sc_diff + brel_ref[cc * gp + g] for g in range(gp)])
    softmax_update(s, m_c, l_c, acc_c, b_dv)

    lf = b_lf[slot]
    f1, f2, f3 = _split3(lf.reshape(gp * SUBLANE, LANE))
    within = (_dot(f1, upper) + _dot(f2, upper) + _dot(f3, upper)).reshape(gp, SUBLANE, LANE)
    off = suffix_offsets(car_d, jnp.sum(lf, axis=2, keepdims=True))
    s = jnp.stack([_dot(qf, page(b_fk, g)) for g in range(gp)]) * sc_sb + within + off
    softmax_update(s, m_d, l_d, acc_d, b_fv)

    @pl.when(c == nc - 1)
    def _():
        oa_ref[...] = acc_a[...]
        o_b = acc_b[...] / l_b[...]
        ob_ref[...] = _dot(o_b.astype(BF16), wuv_ref[...])
        o_c = acc_c[...] / l_c[...]
        oc_ref[...] = o_c - _lambda(lam_ref, lam_init) * pltpu.roll(o_c, N_HEADS, 0)
        od_ref[...] = acc_d[...] / l_d[...]


def _decode(pt, qs, news, brel, bnew, lam_qk, wuv_all, caches, layer, lam_init, gp):
    nseq, n_pages = pt.shape
    nc = n_pages // gp
    total = nseq * nc
    qa, ql, qr, qd, qf = qs

    def per_seq(a):
        return pl.BlockSpec((None,) + a.shape[1:], lambda i, pt: (i // nc,) + (0,) * (a.ndim - 1))

    def whole(a):
        return pl.BlockSpec(a.shape, lambda i, pt: (0,) * a.ndim)

    hbm = pl.BlockSpec(memory_space=pl.ANY)
    feats = [c.shape[2] for c in caches]
    bufs = [pltpu.VMEM((2, gp, f, LANE), F32) for f in feats[:8]] + [pltpu.VMEM((2, gp, SUBLANE, LANE), F32)]
    col = pltpu.VMEM((SUBLANE, 1), F32)
    state = [pltpu.VMEM((SUBLANE, HEAD_W), F32), col,
             col, col, pltpu.VMEM((SUBLANE, MLA_KV_RANK), F32),
             col, col, pltpu.VMEM((SUBLANE, HEAD_W), F32),
             col, col, pltpu.VMEM((SUBLANE, HEAD_W), F32), col]
    grid_spec = pltpu.PrefetchScalarGridSpec(
        num_scalar_prefetch=1,
        grid=(total,),
        in_specs=[per_seq(a) for a in qs] + [per_seq(a) for a in news]
        + [whole(brel), whole(bnew), whole(lam_qk), whole(wuv_all)] + [hbm] * 9,
        out_specs=[pl.BlockSpec((None, SUBLANE, HEAD_W), lambda i, pt: (i // nc, 0, 0)),
                   pl.BlockSpec((None, SUBLANE, 256), lambda i, pt: (i // nc, 0, 0)),
                   pl.BlockSpec((None, SUBLANE, HEAD_W), lambda i, pt: (i // nc, 0, 0)),
                   pl.BlockSpec((None, SUBLANE, HEAD_W), lambda i, pt: (i // nc, 0, 0))],
        scratch_shapes=bufs + [pltpu.SemaphoreType.DMA((2,))] + state,
    )
    return pl.pallas_call(
        functools.partial(_dec_body, layer=layer, gp=gp, n_pages=n_pages, nc=nc, total=total, lam_init=lam_init),
        grid_spec=grid_spec,
        out_shape=[jax.ShapeDtypeStruct((nseq, SUBLANE, HEAD_W), F32),
                   jax.ShapeDtypeStruct((nseq, SUBLANE, 256), F32),
                   jax.ShapeDtypeStruct((nseq, SUBLANE, HEAD_W), F32),
                   jax.ShapeDtypeStruct((nseq, SUBLANE, HEAD_W), F32)],
        compiler_params=_cparams(("arbitrary",)),
        name="decode_attn",
    )(pt.reshape(-1), *qs, *news, brel, bnew, lam_qk, wuv_all, *caches)


def _t5_bucket(rel):
    n = jnp.maximum(rel, 0)
    exact = N_BUCKETS // 2
    log_ratio = jnp.log(jnp.maximum(n, 1).astype(F32) / exact) / math.log(REL_MAX_DIST / exact)
    far = exact + (log_ratio * (N_BUCKETS - exact)).astype(I32)
    return jnp.where(n < exact, n, jnp.minimum(far, N_BUCKETS - 1))


def _rope_tables(pos):
    half = MLA_ROPE // 2
    inv = ROPE_THETA ** (-jnp.arange(half, dtype=F32) / half)
    ang = pos.astype(F32)[:, None] * inv[None, :]
    cos, sin = jnp.cos(ang), jnp.sin(ang)
    z = jnp.zeros((pos.shape[0], ROPE_LO), F32)
    z2 = jnp.zeros((pos.shape[0], LANE - ROPE_HI), F32)
    return jnp.concatenate([z, cos, cos, z2], axis=1), jnp.concatenate([z, -sin, sin, z2], axis=1)


def _layer_params(l, w_in, g_attn, g_cq, w_uq, g_ckv, w_uk, w_uv, b_f, g_grp, g_subln, w_out, g_ffn, lam_init):
    w = w_in[l]
    d = w.shape[0]
    o = np.cumsum((0, 256, 64, 64, 192, 128, 32, 256, 64, 64, 256, 64, 64, 4)).tolist()
    qa, ka, va, cq, ckv, kr, qc, kc, vc, qd, kd, vd, fd = [w[:, o[k]:o[k + 1]] for k in range(13)]
    pad = jnp.zeros((d, 256 - MLA_Q_RANK - MLA_ROPE - N_HEADS), w.dtype)
    w_p = jnp.concatenate([qa, cq, kr, fd, pad, qc, qd, ka, va, ckv, kc, vc, kd, vd], axis=1).astype(BF16)
    uq = w_uq[l]
    nope = uq[:, :, :MLA_NOPE].reshape(MLA_Q_RANK, N_HEADS * MLA_NOPE)
    rope = jnp.pad(uq[:, :, MLA_NOPE:], ((0, 0), (0, 0), (ROPE_LO, LANE - ROPE_HI))).reshape(MLA_Q_RANK, N_HEADS * LANE)
    wuq = jnp.pad(jnp.concatenate([nope, rope], axis=1), ((0, 256 - MLA_Q_RANK), (0, 0))).astype(BF16)
    uk = w_uk[l]
    wuk = jnp.zeros((N_HEADS * MLA_NOPE, N_HEADS * MLA_KV_RANK), F32)
    for h in range(N_HEADS):
        wuk = wuk.at[h * MLA_NOPE:(h + 1) * MLA_NOPE, h * MLA_KV_RANK:(h + 1) * MLA_KV_RANK].set(uk[:, h, :].T)
    uv = w_uv[l]
    wuv_pad = jnp.stack([jnp.pad(uv[:, h, :], ((0, 0), (h * HEAD_W, (N_HEADS - 1 - h) * HEAD_W)))
                         for h in range(N_HEADS)]).astype(BF16)
    wuv_all = uv.reshape(MLA_KV_RANK, N_HEADS * HEAD_W).astype(BF16)
    return dict(
        g_attn=g_attn[l][None], w_p=w_p,
        gcq=jnp.pad(g_cq[l], (0, 256 - MLA_Q_RANK))[None], gckv=g_ckv[l][None],
        wuq=wuq, wuk=wuk.astype(BF16), wuv_pad=wuv_pad, wuv_all=wuv_all,
        bfp=jnp.pad(b_f[l], (LOGF_LO, LANE - LOGF_HI))[None],
        ggrp=g_grp[l], gsub=(jnp.tile(g_subln[l], N_HEADS) * (1.0 - lam_init))[None],
        w4=w_out[l].reshape(N_HEADS, 256, d).astype(BF16), g_ffn=g_ffn[l][None])


def _mixer_rows(p, kmla, lead):
    def cut(a, lo, w, *feat):
        return a[:, lo:lo + w].reshape(lead + (feat if feat else (w,)))
    return (cut(p, C_KAVA, 64), cut(p, C_KAVA + 64, 64), cut(kmla, 0, MLA_KV_RANK),
            cut(kmla, LANE + ROPE_LO, MLA_ROPE), cut(p, C_KCVC, 64, 2, DIFF_DIM), cut(p, C_KCVC + 64, 64),
            cut(p, C_KDVD, 64), cut(p, C_KDVD + 64, 64), cut(kmla, LANE + LOGF_LO, N_HEADS))


def _pick(n, pref):
    t = min(n, pref)
    assert n % t == 0, (n, pref)
    return t


def kernel(x_prompt, x_sample, cache_sb_k, cache_sb_v, cache_mla_ckv, cache_mla_kr, cache_diff_k, cache_diff_v,
           cache_fox_k, cache_fox_v, cache_fox_logf, page_table, g_attn, w_in, g_cq, w_uq, g_ckv, w_uk, w_uv,
           lam_qk, g_subln, b_f, g_grp, w_out, rel_table, g_ffn, w1_dense, w3_dense, w2_dense, w_router,
           w1_exp, w3_exp, w2_exp, g_final):
    bsz, t_len, d = x_prompt.shape
    nseq, dec_t, _ = x_sample.shape
    assert dec_t == 1
    depth = w_in.shape[0]
    n_pages = page_table.shape[1]
    page = cache_sb_k.shape[2]
    assert page == LANE
    past = n_pages * page
    n_p = bsz * t_len

    tq = _pick(t_len, 256)
    tm = _pick(n_p, 512)
    ts = _pick(nseq, 128)
    gp = _pick(n_pages, 16)

    def tr(c):
        return jnp.swapaxes(c.reshape(c.shape[:3] + (-1,)), 2, 3)
    dk = jnp.transpose(cache_diff_k, (0, 1, 3, 4, 2)).reshape(cache_diff_k.shape[:2] + (2 * DIFF_DIM, page))
    caches = (tr(cache_sb_k), tr(cache_sb_v), cache_mla_ckv, tr(cache_mla_kr), dk, tr(cache_diff_v),
              tr(cache_fox_k), tr(cache_fox_v), tr(cache_fox_logf))

    tab_c_p, tab_s_p = _rope_tables(jnp.arange(t_len, dtype=I32))
    tab_c_s, tab_s_s = _rope_tables(jnp.full((ts,), past, I32))
    ltri = jnp.tril(jnp.ones((t_len, t_len), BF16))

    assert tq + 1 >= REL_MAX_DIST
    rr = jnp.arange(tq, dtype=I32)
    bias01 = jnp.stack([
        jnp.moveaxis(rel_table[_t5_bucket(rr[:, None] - rr[None, :] + dl * tq)], -1, 0).reshape(N_HEADS * tq, tq)
        for dl in range(2)])
    bfar = jnp.repeat(rel_table[N_BUCKETS - 1], tq)[:, None]
    brow = rel_table[_t5_bucket(past - jnp.arange(past, dtype=I32))].T
    brel = jnp.tile(brow, (2, 1)).reshape(SUBLANE, n_pages, page).transpose(1, 0, 2)
    bnew = jnp.tile(rel_table[_t5_bucket(jnp.zeros((), I32))], 2)[:, None]

    xp = x_prompt.reshape(n_p, d)
    xs = x_sample.reshape(nseq, d)
    rows_p, rows_s = [], []
    for l in range(depth):
        lam_init = 0.8 - 0.6 * math.exp(-0.3 * l)
        prm = _layer_params(l, w_in, g_attn, g_cq, w_uq, g_ckv, w_uk, w_uv, b_f, g_grp, g_subln, w_out, g_ffn,
                            lam_init)
        lam_l = lam_qk[l]

        p = _proj(xp, prm["g_attn"], prm["w_p"], tm)
        qmla, kmla = _prep(p, prm["gcq"], prm["gckv"], prm["wuq"], prm["wuk"], tab_c_p, tab_s_p, prm["bfp"], tm)
        ccol, crow = _cum(kmla, ltri, bsz, t_len, tq)
        oa = _sb_attn(p, bsz, t_len, tq)
        ob = _mla_attn(qmla, kmla, prm["wuv_pad"], bsz, t_len, tq)
        oc = _diff_attn(lam_l, p, bias01, bfar, bsz, t_len, tq, lam_init)
        od = _fox_attn(p, ccol, crow, bsz, t_len, tq)
        xp = _post(oa, ob, oc, od, xp, prm["ggrp"], prm["gsub"], prm["w4"], tm)
        rows_p.append(_mixer_rows(p, kmla, (bsz, t_len)))

        ps = _proj(xs, prm["g_attn"], prm["w_p"], ts)
        qmla_s, kmla_s = _prep(ps, prm["gcq"], prm["gckv"], prm["wuq"], prm["wuk"], tab_c_s, tab_s_s, prm["bfp"], ts)
        rs = _mixer_rows(ps, kmla_s, (nseq, 1))
        rows_s.append(rs)

        def heads8(a):
            return jnp.pad(a, ((0, 0), (0, SUBLANE - N_HEADS), (0, 0))).astype(BF16)
        qm = qmla_s.reshape(nseq, N_HEADS, 256)
        qc = ps[:, C_QC:C_QC + 256].reshape(nseq, N_HEADS, 2, DIFF_DIM)
        zc = jnp.zeros_like(qc[:, :, 0])
        qd8 = jnp.concatenate([jnp.concatenate([qc[:, :, 0], zc], axis=-1),
                               jnp.concatenate([zc, qc[:, :, 1]], axis=-1)], axis=1).astype(BF16)
        qs = (heads8(ps[:, C_QA:C_QA + 256].reshape(nseq, N_HEADS, HEAD_W)),
              jnp.pad(qm[:, :, :MLA_KV_RANK], ((0, 0), (0, 4), (0, 0))),
              jnp.pad(qm[:, :, LANE + ROPE_LO:LANE + ROPE_HI], ((0, 0), (0, 4), (0, 0))),
              qd8,
              heads8(ps[:, C_QD:C_QD + 256].reshape(nseq, N_HEADS, HEAD_W)))
        _, _, ckv_n, kr_n, kc_n, vc_n, kd_n, vd_n, lf_n = rs
        news = (ckv_n, kr_n, kc_n.reshape(nseq, 1, 2 * DIFF_DIM), vc_n, kd_n, vd_n,
                jnp.pad(jnp.swapaxes(lf_n, 1, 2), ((0, 0), (0, SUBLANE - N_HEADS), (0, 0))))
        oa8, ob8, oc8, od8 = _decode(page_table, qs, news, brel, bnew, lam_l, prm["wuv_all"], caches, l, lam_init, gp)
        oa_s = oa8[:, :N_HEADS].reshape(nseq, 256)
        ob_s = jnp.concatenate([ob8[:, h, h * HEAD_W:(h + 1) * HEAD_W] for h in range(N_HEADS)], axis=1)
        oc_s = oc8[:, :N_HEADS].reshape(nseq, 256)
        od_s = od8[:, :N_HEADS].reshape(nseq, 256)
        xs = _post(oa_s, ob_s, oc_s, od_s, xs, prm["ggrp"], prm["gsub"], prm["w4"], ts)

        i = l // 2
        if l % 2 == 0:
            w1, w3, w2 = w1_dense[i].astype(BF16), w3_dense[i].astype(BF16), w2_dense[i].astype(BF16)
            tf = _pick(w1.shape[1], 1408)
            xp = _ffn(xp, prm["g_ffn"], w1, w3, w2, tm, tf)
            xs = _ffn(xs, prm["g_ffn"], w1, w3, w2, ts, tf)
        else:
            w1, w3, w2 = w1_exp[i].astype(BF16), w3_exp[i].astype(BF16), w2_exp[i].astype(BF16)
            wr = jnp.pad(w_router[i], ((0, 0), (0, LANE - N_EXPERTS)))
            wr3 = jnp.stack(_split3(wr))
            tf = _pick(w1.shape[2], 512)
            xp = _moe(xp, prm["g_ffn"], wr3, w1, w3, w2, tm, _pick(n_p, 1024), tf, _pick(n_p, 256))
            xs = _moe(xs, prm["g_ffn"], wr3, w1, w3, w2, ts, 128, tf, ts)

    y_p = _final_norm(xp, g_final[None], tm).reshape(bsz, t_len, d)
    y_s = _final_norm(xs, g_final[None], ts).reshape(nseq, 1, d)
    outs = [y_p, y_s]
    for k in range(9):
        outs.append(jnp.stack([r[k] for r in rows_p]))
        outs.append(jnp.stack([r[k] for r in rows_s]))
    return tuple(outs)
```
